```python
import math
import jax, jax.numpy as jnp
from jax import lax
import numpy as np

D_MODEL = 1024
BATCH = 2
SEQ = 8192
DEPTH = 2

CHUNK = 64
Q_BLOCK = 128
HEAD_DIM = 64
ROT_FRAC = 4
ROPE_THETA = 500000.0
NORM_EPS = 1e-6
NEG_INF = -1e30

A_HEADS = 8
IDX_HEADS = 8
IDX_DIM = 32
TOPK_MAX = 256
B_HEADS = 4
B_V_DIM = 2 * HEAD_DIM
C_HEADS = D_MODEL // HEAD_DIM
D_FF = 2816
CONV_WIDTH = 3

A_WIDTH = A_HEADS * HEAD_DIM
B_QK_WIDTH = B_HEADS * 2 * HEAD_DIM
B_WIDTH = B_HEADS * B_V_DIM
EVEN_SIZES = (A_WIDTH, A_WIDTH, A_WIDTH, IDX_HEADS * IDX_DIM, IDX_DIM, IDX_HEADS,
              B_QK_WIDTH, B_QK_WIDTH, B_WIDTH)
EVEN_IN = sum(EVEN_SIZES)
ODD_SIZES = (D_MODEL, D_MODEL, D_MODEL, C_HEADS)
ODD_IN = sum(ODD_SIZES)
N_EVEN = (DEPTH + 1) // 2
N_ODD = DEPTH // 2

kernel_name = 'hybrid_dsa_diff_fox_convffn'

F32 = jnp.float32


def _rms(x, g):
    xf = x.astype(F32)
    y = xf * lax.rsqrt(jnp.mean(xf * xf, axis=-1, keepdims=True) + NORM_EPS)
    return (y * g.astype(F32)).astype(x.dtype)


def _split(h, sizes):
    cuts = [int(c) for c in np.cumsum(sizes)[:-1]]
    return jnp.split(h, cuts, axis=-1)


def _rope(x, pos):
    rot = x.shape[-1] // ROT_FRAC
    half = rot // 2
    inv = ROPE_THETA ** (-jnp.arange(half, dtype=F32) / half)
    ang = pos.astype(F32)[:, None] * inv[None, :]
    cos = jnp.cos(ang)[:, None, :]
    sin = jnp.sin(ang)[:, None, :]
    xr = x[..., :rot].astype(F32)
    x1, x2 = xr[..., :half], xr[..., half:]
    r = jnp.concatenate([x1 * cos - x2 * sin, x2 * cos + x1 * sin], axis=-1)
    return jnp.concatenate([r.astype(x.dtype), x[..., rot:]], axis=-1)


def _to_blocks(a):
    b, s = a.shape[:2]
    a = a.reshape((b, s // Q_BLOCK, Q_BLOCK) + a.shape[2:])
    return jnp.moveaxis(a, 1, 0)


def _from_blocks(a):
    a = jnp.moveaxis(a, 0, 1)
    return a.reshape((a.shape[0], -1) + a.shape[3:])


def _dsa_attention(q, k, v, qi, ki, wi, pos):
    seq = q.shape[1]
    topk = min(TOPK_MAX, seq // 4)
    key_chunk = pos // CHUNK
    scale = HEAD_DIM ** -0.5

    def block(args):
        qb, qib, wib, pb = args
        qc = pb // CHUNK
        adm = key_chunk[None, :] <= qc[:, None]
        idx_logit = jnp.einsum('bqhd,bsd->bqhs', qib, ki, preferred_element_type=F32)
        score = jnp.einsum('bqhs,bqh->bqs', jax.nn.relu(idx_logit), wib.astype(F32))
        score = jnp.where(adm[None], score, NEG_INF)
        _, sel = lax.top_k(score, topk)
        valid = key_chunk[sel] <= qc[None, :, None]
        kg = jax.vmap(lambda kb, ib: kb[ib])(k, sel)
        vg = jax.vmap(lambda vb, ib: vb[ib])(v, sel)
        logit = jnp.einsum('bqhd,bqkhd->bhqk', qb, kg, preferred_element_type=F32) * scale
        logit = jnp.where(valid[:, None], logit, NEG_INF)
        p = jax.nn.softmax(logit, axis=-1).astype(v.dtype)
        return jnp.einsum('bhqk,bqkhd->bqhd', p, vg)

    out = lax.map(block, (_to_blocks(q), _to_blocks(qi), _to_blocks(wi),
                          pos.reshape(-1, Q_BLOCK)))
    return _from_blocks(out)


def _diff_attention(q, k, v, lam, pos):
    key_chunk = pos // CHUNK
    scale = HEAD_DIM ** -0.5

    def block(args):
        qb, pb = args
        adm = key_chunk[None, :] <= (pb // CHUNK)[:, None]
        logit = jnp.einsum('bqhmd,bshmd->bhmqs', qb, k, preferred_element_type=F32) * scale
        logit = jnp.where(adm, logit, NEG_INF)
        p = jax.nn.softmax(logit, axis=-1)
        w = p[:, :, 0] - lam * p[:, :, 1]
        return jnp.einsum('bhqs,bshd->bqhd', w.astype(v.dtype), v)

    out = lax.map(block, (_to_blocks(q), pos.reshape(-1, Q_BLOCK)))
    return _from_blocks(out)


def _forgetting_attention(q, k, v, log_f, pos):
    scale = HEAD_DIM ** -0.5
    c = jnp.cumsum(log_f, axis=1)
    c_keys = jnp.moveaxis(c, 1, 2)

    def block(args):
        qb, cb, pb = args
        causal = pos[None, :] <= pb[:, None]
        bias = jnp.moveaxis(cb, 1, 2)[..., None] - c_keys[:, :, None, :]
        logit = jnp.einsum('bqhd,bshd->bhqs', qb, k, preferred_element_type=F32) * scale + bias
        logit = jnp.where(causal, logit, NEG_INF)
        p = jax.nn.softmax(logit, axis=-1).astype(v.dtype)
        return jnp.einsum('bhqs,bshd->bqhd', p, v)

    out = lax.map(block, (_to_blocks(q), _to_blocks(c), pos.reshape(-1, Q_BLOCK)))
    return _from_blocks(out)


def _even_mixer(h, pos, layer, w_in, w_out, a_qn, a_kn, idx_kn, b_qn, b_kn,
                lq1, lk1, lq2, lk2, b_subln):
    b, s, _ = h.shape
    aq, ak, av, iq, ik, iw, bq, bk, bv = _split(h @ w_in, EVEN_SIZES)
    aq = _rope(_rms(aq.reshape(b, s, A_HEADS, HEAD_DIM), a_qn), pos)
    ak = _rope(_rms(ak.reshape(b, s, A_HEADS, HEAD_DIM), a_kn), pos)
    av = av.reshape(b, s, A_HEADS, HEAD_DIM)
    iq = _rope(iq.reshape(b, s, IDX_HEADS, IDX_DIM), pos)
    ik = _rope(_rms(ik, idx_kn)[:, :, None, :], pos)[:, :, 0]
    iw = iw * (IDX_HEADS ** -0.5 * IDX_DIM ** -0.5)
    a_out = _dsa_attention(aq, ak, av, iq, ik, iw, pos).reshape(b, s, A_WIDTH)
    lam_init = 0.8 - 0.6 * math.exp(-0.3 * layer)
    lam = (jnp.exp(jnp.sum(lq1.astype(F32) * lk1.astype(F32)))
           - jnp.exp(jnp.sum(lq2.astype(F32) * lk2.astype(F32))) + lam_init)
    bq = _rope(_rms(bq.reshape(b, s, 2 * B_HEADS, HEAD_DIM), b_qn), pos).reshape(b, s, B_HEADS, 2, HEAD_DIM)
    bk = _rope(_rms(bk.reshape(b, s, 2 * B_HEADS, HEAD_DIM), b_kn), pos).reshape(b, s, B_HEADS, 2, HEAD_DIM)
    bv = bv.reshape(b, s, B_HEADS, B_V_DIM)
    b_out = _rms(_diff_attention(bq, bk, bv, lam, pos), b_subln) * (1.0 - lam_init)
    b_out = b_out.reshape(b, s, B_WIDTH)
    return jnp.concatenate([a_out, b_out], axis=-1) @ w_out


def _odd_mixer(h, pos, w_in, b_f, w_out, c_qn, c_kn):
    b, s, _ = h.shape
    q, k, v, fg = _split(h @ w_in, ODD_SIZES)
    q = _rms(q.reshape(b, s, C_HEADS, HEAD_DIM), c_qn)
    k = _rms(k.reshape(b, s, C_HEADS, HEAD_DIM), c_kn)
    v = v.reshape(b, s, C_HEADS, HEAD_DIM)
    log_f = jax.nn.log_sigmoid((fg + b_f).astype(F32))
    out = _forgetting_attention(q, k, v, log_f, pos)
    return out.reshape(b, s, D_MODEL) @ w_out


def _conv_ffn(h, w_up, w_conv, b_conv, w_down):
    u = h @ w_up
    s = u.shape[1]
    up = jnp.pad(u, ((0, 0), (CONV_WIDTH - 1, 0), (0, 0)))
    conv = b_conv
    for j in range(CONV_WIDTH):
        conv = conv + up[:, j:j + s] * w_conv[j]
    g, val = jnp.split(conv, 2, axis=-1)
    return (jax.nn.silu(g) * val) @ w_down


def setup_inputs(seed: int = 0) -> dict:
    key = jax.random.key(seed)
    ks = iter(jax.random.split(key, 32))

    def nrm(shape, scale):
        return jax.random.normal(next(ks), shape, F32) * scale

    def gain(shape):
        return 1.0 + nrm(shape, 0.05)

    return {
        'x': nrm((BATCH, SEQ, D_MODEL), 1.0),
        'ln_mix': gain((DEPTH, D_MODEL)),
        'ln_ffn': gain((DEPTH, D_MODEL)),
        'ev_w_in': nrm((N_EVEN, D_MODEL, EVEN_IN), D_MODEL ** -0.5),
        'ev_w_out': nrm((N_EVEN, A_WIDTH + B_WIDTH, D_MODEL), (A_WIDTH + B_WIDTH) ** -0.5),
        'ev_a_qnorm': gain((N_EVEN, HEAD_DIM)),
        'ev_a_knorm': gain((N_EVEN, HEAD_DIM)),
        'ev_idx_knorm': gain((N_EVEN, IDX_DIM)),
        'ev_b_qnorm': gain((N_EVEN, HEAD_DIM)),
        'ev_b_knorm': gain((N_EVEN, HEAD_DIM)),
        'ev_lam_q1': nrm((N_EVEN, HEAD_DIM), 0.1),
        'ev_lam_k1': nrm((N_EVEN, HEAD_DIM), 0.1),
        'ev_lam_q2': nrm((N_EVEN, HEAD_DIM), 0.1),
        'ev_lam_k2': nrm((N_EVEN, HEAD_DIM), 0.1),
        'ev_b_subln': gain((N_EVEN, B_V_DIM)),
        'od_w_in': nrm((N_ODD, D_MODEL, ODD_IN), D_MODEL ** -0.5),
        'od_b_f': 3.0 + nrm((N_ODD, C_HEADS), 0.5),
        'od_w_out': nrm((N_ODD, D_MODEL, D_MODEL), D_MODEL ** -0.5),
        'od_c_qnorm': gain((N_ODD, HEAD_DIM)),
        'od_c_knorm': gain((N_ODD, HEAD_DIM)),
        'ffn_up': nrm((DEPTH, D_MODEL, 2 * D_FF), D_MODEL ** -0.5),
        'ffn_conv': nrm((DEPTH, CONV_WIDTH, 2 * D_FF), CONV_WIDTH ** -0.5),
        'ffn_conv_b': nrm((DEPTH, 2 * D_FF), 0.01),
        'ffn_down': nrm((DEPTH, D_FF, D_MODEL), D_FF ** -0.5),
    }


def reference(x, ln_mix, ln_ffn, ev_w_in, ev_w_out, ev_a_qnorm, ev_a_knorm, ev_idx_knorm,
              ev_b_qnorm, ev_b_knorm, ev_lam_q1, ev_lam_k1, ev_lam_q2, ev_lam_k2, ev_b_subln,
              od_w_in, od_b_f, od_w_out, od_c_qnorm, od_c_knorm,
              ffn_up, ffn_conv, ffn_conv_b, ffn_down):
    seq = x.shape[1]
    pos = jnp.arange(seq, dtype=jnp.int32)
    for i in range(DEPTH):
        j = i // 2
        h = _rms(x, ln_mix[i])
        if i % 2 == 0:
            mix = _even_mixer(h, pos, i, ev_w_in[j], ev_w_out[j], ev_a_qnorm[j], ev_a_knorm[j],
                              ev_idx_knorm[j], ev_b_qnorm[j], ev_b_knorm[j], ev_lam_q1[j],
                              ev_lam_k1[j], ev_lam_q2[j], ev_lam_k2[j], ev_b_subln[j])
        else:
            mix = _odd_mixer(h, pos, od_w_in[j], od_b_f[j], od_w_out[j], od_c_qnorm[j], od_c_knorm[j])
        x = x + mix
        x = x + _conv_ffn(_rms(x, ln_ffn[i]), ffn_up[i], ffn_conv[i], ffn_conv_b[i], ffn_down[i])
    return x
```

```python
import functools
import math

import jax
import jax.numpy as jnp
import numpy as np
from jax import lax
from jax.experimental import pallas as pl
from jax.experimental.pallas import tpu as pltpu

F32 = jnp.float32
BF16 = jnp.bfloat16
I32 = jnp.int32

CHUNK = 64
HEAD_DIM = 64
ROT_FRAC = 4
ROPE_THETA = 500000.0
NORM_EPS = 1e-6
NEG_INF = -1e30
A_HEADS = 8
IDX_HEADS = 8
IDX_DIM = 32
TOPK_MAX = 256
B_HEADS = 4
CONV_WIDTH = 3

LANES = 128
INT_MIN = -2147483648
VMEM_LIMIT = 56 * 1024 * 1024

_NT = (((1,), (1,)), ((), ()))


def _params(sem):
    return pltpu.CompilerParams(dimension_semantics=sem, vmem_limit_bytes=VMEM_LIMIT)


def _rms_matmul_body(x_ref, g_ref, w_ref, o_ref, h_ref):
    @pl.when(pl.program_id(1) == 0)
    def _():
        x = x_ref[...]
        ms = jnp.mean(x * x, axis=-1, keepdims=True)
        h_ref[...] = (x * lax.rsqrt(ms + NORM_EPS) * g_ref[...]).astype(BF16)

    o_ref[...] = jnp.dot(h_ref[...], w_ref[...], preferred_element_type=F32)


def _rms_matmul(x, g, w, tm, tn):
    m, k = x.shape
    n = w.shape[1]
    return pl.pallas_call(
        _rms_matmul_body,
        grid=(m // tm, n // tn),
        in_specs=[
            pl.BlockSpec((tm, k), lambda i, j: (i, 0)),
            pl.BlockSpec((1, k), lambda i, j: (0, 0)),
            pl.BlockSpec((k, tn), lambda i, j: (0, j)),
        ],
        out_specs=pl.BlockSpec((tm, tn), lambda i, j: (i, j)),
        out_shape=jax.ShapeDtypeStruct((m, n), F32),
        scratch_shapes=[pltpu.VMEM((tm, k), BF16)],
        compiler_params=_params(("parallel", "arbitrary")),
    )(x, g, w)


def _proj_residual_body(*refs, n_pairs):
    res_ref = refs[0]
    o_ref = refs[1 + 2 * n_pairs]
    acc = res_ref[...]
    for t in range(n_pairs):
        acc = acc + jnp.dot(refs[1 + 2 * t][...], refs[2 + 2 * t][...], preferred_element_type=F32)
    o_ref[...] = acc


def _proj_residual(res, pairs, tm):
    m, n = res.shape
    in_specs = [pl.BlockSpec((tm, n), lambda i: (i, 0))]
    args = [res]
    for a, w in pairs:
        in_specs.append(pl.BlockSpec((tm, a.shape[1]), lambda i: (i, 0)))
        in_specs.append(pl.BlockSpec(w.shape, lambda i: (0, 0)))
        args += [a, w]
    return pl.pallas_call(
        functools.partial(_proj_residual_body, n_pairs=len(pairs)),
        grid=(m // tm,),
        in_specs=in_specs,
        out_specs=pl.BlockSpec((tm, n), lambda i: (i, 0)),
        out_shape=jax.ShapeDtypeStruct((m, n), F32),
        compiler_params=_params(("parallel",)),
    )(*args)


def _split_bf16(x):
    hi = x.astype(BF16)
    lo = (x - hi.astype(F32)).astype(BF16)
    return hi, lo


def _head_rms(xc, bd, gain):
    hi, lo = _split_bf16(xc * xc)
    ms = jnp.dot(hi, bd, preferred_element_type=F32) + jnp.dot(lo, bd, preferred_element_type=F32)
    return xc * lax.rsqrt(ms + NORM_EPS) * gain


def _rope(y, c, a, b, shift):
    return y * c + pltpu.roll(y, LANES - shift, 1) * a + pltpu.roll(y, shift, 1) * b


def _rope_tables(seq, head, reps):
    rot = head // ROT_FRAC
    half = rot // 2
    pos = jnp.arange(seq, dtype=jnp.int32)
    inv = ROPE_THETA ** (-jnp.arange(half, dtype=F32) / half)
    ang = pos.astype(F32)[:, None] * inv[None, :]
    cos, sin = jnp.cos(ang), jnp.sin(ang)
    ones = jnp.ones((seq, head - rot), F32)
    zeros = jnp.zeros((seq, head - rot), F32)
    zh = jnp.zeros((seq, half), F32)
    c = jnp.concatenate([cos, cos, ones], axis=1)
    a = jnp.concatenate([-sin, zh, zeros], axis=1)
    b = jnp.concatenate([zh, sin, zeros], axis=1)
    return tuple(jnp.tile(t, (1, reps)) for t in (c, a, b))


def _prep_even_body(p_ref, c64_ref, a64_ref, b64_ref, c32_ref, a32_ref, b32_ref,
                    aqn_ref, akn_ref, bqn_ref, bkn_ref, ikn_ref, bd_ref,
                    pqh_ref, pql_ref, pkh_ref, pkl_ref,
                    aq_ref, ak_ref, av_ref, bq_ref, bk_ref, bv_ref, iqx_ref, ikx_ref, iw_ref,
                    *, scale, iw_scale):
    bd = bd_ref[...]
    c64, a64, b64 = c64_ref[...], a64_ref[...], b64_ref[...]
    c32, a32, b32 = c32_ref[...], a32_ref[...], b32_ref[...]

    def qk(col0, gain_ref, out_ref, mult):
        gain = gain_ref[...]
        for g in range(4):
            xc = p_ref[:, col0 + LANES * g: col0 + LANES * (g + 1)]
            y = _rope(_head_rms(xc, bd, gain), c64, a64, b64, 8)
            if mult != 1.0:
                y = y * mult
            out_ref[:, LANES * g: LANES * (g + 1)] = y.astype(BF16)

    qk(0, aqn_ref, aq_ref, scale)
    qk(512, akn_ref, ak_ref, 1.0)
    av_ref[...] = p_ref[:, 1024:1536].astype(BF16)
    qk(1536, bqn_ref, bq_ref, scale)
    qk(2048, bkn_ref, bk_ref, 1.0)
    bv_ref[...] = p_ref[:, 2560:3072].astype(BF16)

    g0 = _rope(p_ref[:, 3072:3200], c32, a32, b32, 4)
    g1 = _rope(p_ref[:, 3200:3328], c32, a32, b32, 4)
    g2 = p_ref[:, 3328:3456]
    lane = lax.broadcasted_iota(I32, g2.shape, 1)
    is_k = lane < IDX_DIM
    ms = jnp.sum(jnp.where(is_k, g2 * g2, 0.0), axis=1, keepdims=True) * (1.0 / IDX_DIM)
    g2 = jnp.where(is_k, g2 * lax.rsqrt(ms + NORM_EPS) * ikn_ref[...], g2)
    g2 = _rope(g2, jnp.where(is_k, c32, 1.0), jnp.where(is_k, a32, 0.0), jnp.where(is_k, b32, 0.0), 4)
    iw_ref[...] = g2 * iw_scale
    hi, lo = _split_bf16(jnp.concatenate([g0, g1, g2], axis=1))
    iqx_ref[...] = (jnp.dot(hi, pqh_ref[...], preferred_element_type=F32)
                    + jnp.dot(lo, pql_ref[...], preferred_element_type=F32)).astype(BF16)
    ikx_ref[...] = (jnp.dot(hi, pkh_ref[...], preferred_element_type=F32)
                    + jnp.dot(lo, pkl_ref[...], preferred_element_type=F32)).astype(BF16)


def _index_placement():
    pqh = np.zeros((384, IDX_HEADS * LANES), np.float32)
    pql = np.zeros((384, IDX_HEADS * LANES), np.float32)
    for h in range(IDX_HEADS):
        for d in range(IDX_DIM):
            pqh[IDX_DIM * h + d, LANES * h + d] = 1.0
            pqh[IDX_DIM * h + d, LANES * h + IDX_DIM + d] = 1.0
            pql[IDX_DIM * h + d, LANES * h + 2 * IDX_DIM + d] = 1.0
    pkh = np.zeros((384, LANES), np.float32)
    pkl = np.zeros((384, LANES), np.float32)
    for d in range(IDX_DIM):
        pkh[256 + d, d] = 1.0
        pkh[256 + d, 2 * IDX_DIM + d] = 1.0
        pkl[256 + d, IDX_DIM + d] = 1.0
    return tuple(jnp.asarray(t, BF16) for t in (pqh, pql, pkh, pkl))


def _block_diag_mean():
    r = np.arange(LANES)
    return jnp.asarray((r[:, None] // HEAD_DIM == r[None, :] // HEAD_DIM) / HEAD_DIM, BF16)


def _prep_even(proj, seq, tabs64, tabs32, aqn, akn, bqn, bkn, ikn, tm):
    m = proj.shape[0]
    nt = seq // tm
    row = lambda w: pl.BlockSpec((tm, w), lambda i: (i, 0))
    tab = pl.BlockSpec((tm, LANES), lambda i: (i % nt, 0))
    full = lambda a: pl.BlockSpec(a.shape, lambda i: (0, 0))
    bd = _block_diag_mean()
    place = _index_placement()
    small = [aqn, akn, bqn, bkn, ikn, bd, *place]
    out_w = [512] * 6 + [IDX_HEADS * LANES, LANES, LANES]
    out_dt = [BF16] * 8 + [F32]
    return pl.pallas_call(
        functools.partial(_prep_even_body, scale=HEAD_DIM ** -0.5,
                          iw_scale=IDX_HEADS ** -0.5 * IDX_DIM ** -0.5),
        grid=(m // tm,),
        in_specs=[row(proj.shape[1])] + [tab] * 6 + [full(a) for a in small],
        out_specs=[row(w) for w in out_w],
        out_shape=[jax.ShapeDtypeStruct((m, w), dt) for w, dt in zip(out_w, out_dt)],
        compiler_params=_params(("parallel",)),
    )(proj, *tabs64, *tabs32, *small)


def _prep_odd_body(p_ref, qn_ref, kn_ref, bf_ref, bd_ref, q_ref, k_ref, v_ref, c_ref, ct_ref,
                   carry_ref, *, scale, nt, tm):
    bd = bd_ref[...]
    qn, kn = qn_ref[...], kn_ref[...]
    for g in range(8):
        sl = slice(LANES * g, LANES * (g + 1))
        q_ref[:, sl] = (_head_rms(p_ref[:, sl], bd, qn) * scale).astype(BF16)
        k_ref[:, sl] = _head_rms(p_ref[:, 1024 + LANES * g: 1024 + LANES * (g + 1)], bd, kn).astype(BF16)
    v_ref[...] = p_ref[:, 2048:3072].astype(BF16)

    @pl.when(pl.program_id(0) % nt == 0)
    def _():
        carry_ref[...] = jnp.zeros_like(carry_ref)

    z = p_ref[:, 3072:3200] + bf_ref[...]
    logf = jnp.minimum(z, 0.0) - jnp.log1p(jnp.exp(-jnp.abs(z)))
    r = lax.broadcasted_iota(I32, (tm, tm), 0)
    cidx = lax.broadcasted_iota(I32, (tm, tm), 1)
    tri = jnp.where(r >= cidx, 1.0, 0.0).astype(BF16)
    p1 = logf.astype(BF16)
    rem = logf - p1.astype(F32)
    p2 = rem.astype(BF16)
    p3 = (rem - p2.astype(F32)).astype(BF16)
    cs = (jnp.dot(tri, p1, preferred_element_type=F32) + jnp.dot(tri, p2, preferred_element_type=F32)
          + jnp.dot(tri, p3, preferred_element_type=F32)) + carry_ref[...]
    carry_ref[...] = cs[tm - 1: tm, :]
    c_ref[...] = cs
    ct_ref[...] = cs.T[:ct_ref.shape[0], :]


def _prep_odd(proj, batch, seq, qn, kn, bf, tm):
    m = proj.shape[0]
    nt = seq // tm
    heads = 1024 // HEAD_DIM
    row = lambda w: pl.BlockSpec((tm, w), lambda i: (i, 0))
    full = lambda a: pl.BlockSpec(a.shape, lambda i: (0, 0))
    bd = _block_diag_mean()
    small = [qn, kn, bf, bd]
    return pl.pallas_call(
        functools.partial(_prep_odd_body, scale=HEAD_DIM ** -0.5, nt=nt, tm=tm),
        grid=(m // tm,),
        in_specs=[row(proj.shape[1])] + [full(a) for a in small],
        out_specs=[row(1024), row(1024), row(1024), row(LANES),
                   pl.BlockSpec((None, heads, tm), lambda i: (i // nt, 0, i % nt))],
        out_shape=[jax.ShapeDtypeStruct((m, 1024), BF16)] * 3
                  + [jax.ShapeDtypeStruct((m, LANES), F32),
                     jax.ShapeDtypeStruct((batch, heads, seq), F32)],
        scratch_shapes=[pltpu.VMEM((1, LANES), F32)],
        compiler_params=_params(("arbitrary",)),
    )(proj, *small)


def _stack_units(q):
    lane = lax.broadcasted_iota(I32, q.shape, 1)
    zero = jnp.zeros_like(q)
    return jnp.concatenate([jnp.where(lane < HEAD_DIM, q, zero), jnp.where(lane >= HEAD_DIM, q, zero)], axis=0)


def _softmax_step(s, v, m_ref, l_ref, acc_ref):
    m_prev = m_ref[...]
    m_new = jnp.maximum(m_prev, jnp.max(s, axis=1, keepdims=True))
    alpha = jnp.exp(m_prev - m_new)
    p = jnp.exp(s - m_new)
    l_ref[...] = alpha * l_ref[...] + jnp.sum(p, axis=1, keepdims=True)
    acc_ref[...] = alpha * acc_ref[...] + jnp.dot(p.astype(BF16), v, preferred_element_type=F32)
    m_ref[...] = m_new


def _flash_body(*refs, mode, t, lam_init):
    if mode == "fox":
        q_ref, k_ref, v_ref, c_ref, ct_ref, o_ref, m_ref, l_ref, acc_ref = refs
    else:
        (q_ref, k_ref, v_ref, lq1_ref, lk1_ref, lq2_ref, lk2_ref, sub_ref,
         o_ref, m_ref, l_ref, acc_ref) = refs
    pair = pl.program_id(1)
    i = pl.program_id(2)
    qs = _stack_units(q_ref[...])
    m_ref[...] = jnp.full(m_ref.shape, NEG_INF, F32)
    l_ref[...] = jnp.zeros(l_ref.shape, F32)
    acc_ref[...] = jnp.zeros(acc_ref.shape, F32)

    if mode == "fox":
        c = c_ref[...]
        lane = lax.broadcasted_iota(I32, c.shape, 1)
        cq0 = jnp.sum(jnp.where(lane == 2 * pair, c, 0.0), axis=1, keepdims=True)
        cq1 = jnp.sum(jnp.where(lane == 2 * pair + 1, c, 0.0), axis=1, keepdims=True)

    row = lax.broadcasted_iota(I32, (t, t), 0)
    col = lax.broadcasted_iota(I32, (t, t), 1)
    if mode == "fox":
        diag_ok = col <= row
    else:
        diag_ok = (col // CHUNK) <= (row // CHUNK)

    def step(j, masked):
        off = pl.multiple_of(j * t, t)
        k = k_ref[pl.ds(off, t), :]
        v = v_ref[pl.ds(off, t), :]
        s = lax.dot_general(qs, k, _NT, preferred_element_type=F32)
        s0, s1 = s[:t], s[t:]
        if mode == "fox":
            s0 = s0 + (cq0 - ct_ref[0:1, pl.ds(off, t)])
            s1 = s1 + (cq1 - ct_ref[1:2, pl.ds(off, t)])
        if masked:
            s0 = jnp.where(diag_ok, s0, NEG_INF)
            s1 = jnp.where(diag_ok, s1, NEG_INF)
        _softmax_step(jnp.concatenate([s0, s1], axis=0), v, m_ref, l_ref, acc_ref)

    def body(j, carry):
        step(j, False)
        return carry

    lax.fori_loop(0, i, body, 0)
    step(i, True)

    o = acc_ref[...] / l_ref[...]
    if mode == "fox":
        lane = lax.broadcasted_iota(I32, (t, LANES), 1)
        o_ref[...] = jnp.where(lane < HEAD_DIM, o[:t], o[t:]).astype(BF16)
    else:
        lam = (jnp.exp(jnp.sum(lq1_ref[...] * lk1_ref[...], axis=1, keepdims=True))
               - jnp.exp(jnp.sum(lq2_ref[...] * lk2_ref[...], axis=1, keepdims=True)) + lam_init)
        d = o[:t] - lam * o[t:]
        ms = jnp.mean(d * d, axis=1, keepdims=True)
        o_ref[...] = (d * lax.rsqrt(ms + NORM_EPS) * sub_ref[...] * (1.0 - lam_init)).astype(BF16)


def _flash(mode, q, k, v, extra, batch, seq, lam_init=0.0, t=256):
    m, width = q.shape
    pairs = width // LANES
    nq = seq // t
    qspec = pl.BlockSpec((t, LANES), lambda b, p, i: (b * nq + i, p))
    kvspec = pl.BlockSpec((seq, LANES), lambda b, p, i: (b, p))
    if mode == "fox":
        c, ct = extra
        in_specs = [qspec, kvspec, kvspec,
                    pl.BlockSpec((t, LANES), lambda b, p, i: (b * nq + i, 0)),
                    pl.BlockSpec((None, 2, seq), lambda b, p, i: (b * pairs + p, 0, 0))]
        args = [q, k, v, c, ct.reshape(batch * pairs, 2, seq)]
    else:
        in_specs = [qspec, kvspec, kvspec] + [pl.BlockSpec(a.shape, lambda b, p, i: (0, 0)) for a in extra]
        args = [q, k, v, *extra]
    return pl.pallas_call(
        functools.partial(_flash_body, mode=mode, t=t, lam_init=lam_init),
        grid=(batch, pairs, nq),
        in_specs=in_specs,
        out_specs=pl.BlockSpec((t, LANES), lambda b, p, i: (b * nq + i, p)),
        out_shape=jax.ShapeDtypeStruct((m, width), BF16),
        scratch_shapes=[pltpu.VMEM((2 * t, 1), F32), pltpu.VMEM((2 * t, 1), F32),
                        pltpu.VMEM((2 * t, LANES), F32)],
        compiler_params=_params(("parallel", "parallel", "arbitrary")),
    )(*args)


def _dsa_body(iqx_ref, iw_ref, aq_ref, ikx_ref, ak_ref, av_ref, o_ref,
              key_ref, m_ref, l_ref, acc_ref, *, tq, tk, topk, idx_bits):
    i = pl.program_id(1)
    nkt = (i * tq) // tk + 1
    fold = tk // LANES

    w = iw_ref[...]

    def score_tile(kt, masked):
        off = pl.multiple_of(kt * tk, tk)
        kx = ikx_ref[pl.ds(off, tk), :]
        sc = jnp.zeros((tq, tk), F32)
        for h in range(IDX_HEADS):
            lg = lax.dot_general(iqx_ref[:, LANES * h: LANES * (h + 1)], kx, _NT,
                                 preferred_element_type=F32)
            sc = sc + jnp.maximum(lg, 0.0) * w[:, IDX_DIM + h: IDX_DIM + h + 1]
        bits = pltpu.bitcast(sc, I32)
        key = jnp.where(bits < 0, bits ^ 0x7FFFFFFF, bits)
        if masked:
            row = lax.broadcasted_iota(I32, (tq, tk), 0)
            col = lax.broadcasted_iota(I32, (tq, tk), 1) + off
            adm = (col // CHUNK) <= ((i * tq + row) // CHUNK)
            key = jnp.where(adm, key, INT_MIN)
        key_ref[:, pl.ds(off, tk)] = key

    def score_body(kt, carry):
        score_tile(kt, False)
        return carry

    lax.fori_loop(0, nkt - 1, score_body, 0)
    score_tile(nkt - 1, True)

    def lane_fold(x):
        part = x[:, :LANES]
        for f in range(1, fold):
            part = part + x[:, LANES * f: LANES * (f + 1)]
        return part

    def count(pred):
        def body(kt, c):
            off = pl.multiple_of(kt * tk, tk)
            return c + lane_fold(jnp.where(pred(key_ref[:, pl.ds(off, tk)], off), 1.0, 0.0))
        c = lax.fori_loop(0, nkt, body, jnp.zeros((tq, LANES), F32))
        return jnp.sum(c, axis=1, keepdims=True)

    def bit_body(b, carry):
        res, cnt_res = carry
        cand = res | jnp.left_shift(jnp.int32(1), 31 - b)
        cand_s = cand ^ INT_MIN
        cnt = count(lambda kk, off: kk >= cand_s)
        ok = cnt >= topk
        return jnp.where(ok, cand, res), jnp.where(ok, cnt, cnt_res)

    res, cnt_res = lax.fori_loop(0, 32, bit_body,
                                 (jnp.zeros((tq, 1), I32), jnp.zeros((tq, 1), F32)))
    thr = jnp.maximum(res ^ INT_MIN, INT_MIN + 1)

    @pl.when(jnp.max(cnt_res) > topk)
    def _():
        need = topk - count(lambda kk, off: kk > thr)

        def idx_of(off):
            return lax.broadcasted_iota(I32, (tq, tk), 1) + off

        def jbody(b, lo):
            cand = lo | jnp.left_shift(jnp.int32(1), idx_bits - 1 - b)
            c = count(lambda kk, off: (kk == thr) & (idx_of(off) < cand))
            return jnp.where(c < need, cand, lo)

        last = lax.fori_loop(0, idx_bits, jbody, jnp.zeros((tq, 1), I32))

        def drop(kt, carry):
            off = pl.multiple_of(kt * tk, tk)
            kk = key_ref[:, pl.ds(off, tk)]
            key_ref[:, pl.ds(off, tk)] = jnp.where((kk == thr) & (idx_of(off) > last), INT_MIN, kk)
            return carry

        lax.fori_loop(0, nkt, drop, 0)

    lane = lax.broadcasted_iota(I32, (tq, LANES), 1)
    for p in range(A_HEADS // 2):
        sl = slice(LANES * p, LANES * (p + 1))
        qs = _stack_units(aq_ref[:, sl])
        m_ref[...] = jnp.full(m_ref.shape, NEG_INF, F32)
        l_ref[...] = jnp.zeros(l_ref.shape, F32)
        acc_ref[...] = jnp.zeros(acc_ref.shape, F32)

        def attend(kt, carry):
            off = pl.multiple_of(kt * tk, tk)
            s = lax.dot_general(qs, ak_ref[pl.ds(off, tk), sl], _NT, preferred_element_type=F32)
            sel = key_ref[:, pl.ds(off, tk)] >= thr
            s = jnp.concatenate([jnp.where(sel, s[:tq], NEG_INF), jnp.where(sel, s[tq:], NEG_INF)], axis=0)
            _softmax_step(s, av_ref[pl.ds(off, tk), sl], m_ref, l_ref, acc_ref)
            return carry

        lax.fori_loop(0, nkt, attend, 0)
        o = acc_ref[...] / l_ref[...]
        o_ref[:, sl] = jnp.where(lane < HEAD_DIM, o[:tq], o[tq:]).astype(BF16)


def _dsa(iqx, iw, aq, ikx, ak, av, batch, seq, tq=128, tk=512):
    m = aq.shape[0]
    nq = seq // tq
    tk = min(tk, seq)
    topk = min(TOPK_MAX, seq // 4)
    qrow = lambda w: pl.BlockSpec((tq, w), lambda b, i: (b * nq + i, 0))
    krow = lambda w: pl.BlockSpec((seq, w), lambda b, i: (b, 0))
    return pl.pallas_call(
        functools.partial(_dsa_body, tq=tq, tk=tk, topk=float(topk),
                          idx_bits=max(1, (seq - 1).bit_length())),
        grid=(batch, nq),
        in_specs=[qrow(iqx.shape[1]), qrow(LANES), qrow(512), krow(LANES), krow(512), krow(512)],
        out_specs=qrow(512),
        out_shape=jax.ShapeDtypeStruct((m, 512), BF16),
        scratch_shapes=[pltpu.VMEM((tq, seq), I32), pltpu.VMEM((2 * tq, 1), F32),
                        pltpu.VMEM((2 * tq, 1), F32), pltpu.VMEM((2 * tq, LANES), F32)],
        compiler_params=_params(("parallel", "arbitrary")),
    )(iqx, iw, aq, ikx, ak, av)


HALO = 8


def _ffn_body(x_ref, halo_ref, g_ref, wg_ref, wv_ref, cg_ref, cv_ref, bg_ref, bv_ref, wd_ref,
              o_ref, h_ref, ug_ref, uv_ref, acc_ref, *, tm, nt):
    i = pl.program_id(0)
    f = pl.program_id(1)

    @pl.when(f == 0)
    def _():
        def norm(x):
            ms = jnp.mean(x * x, axis=-1, keepdims=True)
            return x * lax.rsqrt(ms + NORM_EPS) * g_ref[...]

        first = (i % nt) == 0
        h_ref[0:HALO, :] = jnp.where(first, 0.0, norm(halo_ref[...])).astype(BF16)
        h_ref[HALO:, :] = norm(x_ref[...]).astype(BF16)
        acc_ref[...] = jnp.zeros_like(acc_ref)

    h = h_ref[...]
    ug_ref[...] = jnp.dot(h, wg_ref[...], preferred_element_type=F32)
    uv_ref[...] = jnp.dot(h, wv_ref[...], preferred_element_type=F32)

    def conv(u_ref, c_ref, b_ref):
        out = b_ref[...]
        for j in range(CONV_WIDTH):
            start = HALO - (CONV_WIDTH - 1) + j
            out = out + u_ref[start: start + tm, :] * c_ref[j: j + 1, :]
        return out

    g = conv(ug_ref, cg_ref, bg_ref)
    val = conv(uv_ref, cv_ref, bv_ref)
    act = (g / (1.0 + jnp.exp(-g))) * val
    acc_ref[...] += jnp.dot(act.astype(BF16), wd_ref[...], preferred_element_type=F32)

    @pl.when(f == pl.num_programs(1) - 1)
    def _():
        o_ref[...] = x_ref[...] + acc_ref[...]


def _conv_ffn(x, seq, g, w_up, w_conv, b_conv, w_down, tm, tf):
    m, d = x.shape
    dff = w_down.shape[0]
    nf = dff // tf
    nt = seq // tm
    hb = tm // HALO
    return pl.pallas_call(
        functools.partial(_ffn_body, tm=tm, nt=nt),
        grid=(m // tm, nf),
        in_specs=[
            pl.BlockSpec((tm, d), lambda i, f: (i, 0)),
            pl.BlockSpec((HALO, d), lambda i, f: (jnp.maximum(i * hb - 1, 0), 0)),
            pl.BlockSpec((1, d), lambda i, f: (0, 0)),
            pl.BlockSpec((d, tf), lambda i, f: (0, f)),
            pl.BlockSpec((d, tf), lambda i, f: (0, nf + f)),
            pl.BlockSpec((CONV_WIDTH, tf), lambda i, f: (0, f)),
            pl.BlockSpec((CONV_WIDTH, tf), lambda i, f: (0, nf + f)),
            pl.BlockSpec((1, tf), lambda i, f: (0, f)),
            pl.BlockSpec((1, tf), lambda i, f: (0, nf + f)),
            pl.BlockSpec((tf, d), lambda i, f: (f, 0)),
        ],
        out_specs=pl.BlockSpec((tm, d), lambda i, f: (i, 0)),
        out_shape=jax.ShapeDtypeStruct((m, d), F32),
        scratch_shapes=[pltpu.VMEM((tm + HALO, d), BF16), pltpu.VMEM((tm + HALO, tf), F32),
                        pltpu.VMEM((tm + HALO, tf), F32), pltpu.VMEM((tm, d), F32)],
        compiler_params=_params(("parallel", "arbitrary")),
    )(x, x, g, w_up, w_up, w_conv, w_conv, b_conv, b_conv, w_down)


def _tile2(g):
    return jnp.tile(g.astype(F32), 2).reshape(1, LANES)


def _pick(total, prefs):
    for p in prefs:
        if total % p == 0:
            return p
    return total


def _even_layer(x, batch, seq, layer, ln, w_in, w_out, a_qn, a_kn, idx_kn, b_qn, b_kn,
                lq1, lk1, lq2, lk2, b_subln):
    d = x.shape[1]
    a_w = A_HEADS * HEAD_DIM
    iq_w = IDX_HEADS * IDX_DIM
    b_w = B_HEADS * 2 * HEAD_DIM
    cuts = np.cumsum([a_w, a_w, a_w, iq_w, IDX_DIM, IDX_HEADS, b_w, b_w])
    aq, ak, av, iq, ik, iw, bq, bk, bv = jnp.split(w_in, [int(c) for c in cuts], axis=1)
    pad = jnp.zeros((d, 384 - iq_w - IDX_DIM - IDX_HEADS), w_in.dtype)
    w_cat = jnp.concatenate([aq, ak, av, bq, bk, bv, iq, ik, iw, pad], axis=1).astype(BF16)

    tm = _pick(x.shape[0], (1024, 512, 256))
    proj = _rms_matmul(x, ln.reshape(1, d), w_cat, tm, _pick(w_cat.shape[1], (1152, 384, 128)))

    tabs64 = _rope_tables(seq, HEAD_DIM, LANES // HEAD_DIM)
    tabs32 = _rope_tables(seq, IDX_DIM, LANES // IDX_DIM)
    ikn = jnp.concatenate([idx_kn.astype(F32), jnp.ones((LANES - IDX_DIM,), F32)]).reshape(1, LANES)
    aq_, ak_, av_, bq_, bk_, bv_, iqx, ikx, iw_ = _prep_even(
        proj, seq, tabs64, tabs32, _tile2(a_qn), _tile2(a_kn), _tile2(b_qn), _tile2(b_kn), ikn,
        _pick(seq, (256, 128)))

    a_out = _dsa(iqx, iw_, aq_, ikx, ak_, av_, batch, seq)
    lam_init = 0.8 - 0.6 * math.exp(-0.3 * layer)
    lam_p = [t.astype(F32).reshape(1, HEAD_DIM) for t in (lq1, lk1, lq2, lk2)]
    b_out = _flash("diff", bq_, bk_, bv_, lam_p + [b_subln.astype(F32).reshape(1, LANES)],
                   batch, seq, lam_init=lam_init)
    w_o = w_out.astype(BF16)
    return _proj_residual(x, [(a_out, w_o[:a_w]), (b_out, w_o[a_w:])], _pick(x.shape[0], (512, 256)))


def _odd_layer(x, batch, seq, ln, w_in, b_f, w_out, c_qn, c_kn):
    d = x.shape[1]
    heads = d // HEAD_DIM
    pad = jnp.zeros((d, LANES - heads), w_in.dtype)
    w_cat = jnp.concatenate([w_in, pad], axis=1).astype(BF16)
    tm = _pick(x.shape[0], (1024, 512, 256))
    proj = _rms_matmul(x, ln.reshape(1, d), w_cat, tm, _pick(w_cat.shape[1], (640, 128)))
    bf = jnp.concatenate([b_f.astype(F32), jnp.zeros((LANES - heads,), F32)]).reshape(1, LANES)
    q, k, v, c, ct = _prep_odd(proj, batch, seq, _tile2(c_qn), _tile2(c_kn), bf, _pick(seq, (256, 128)))
    o = _flash("fox", q, k, v, (c, ct), batch, seq)
    return _proj_residual(x, [(o, w_out.astype(BF16))], _pick(x.shape[0], (512, 256)))


def _ffn_layer(x, seq, ln, w_up, w_conv, b_conv, w_down):
    d = x.shape[1]
    return _conv_ffn(x, seq, ln.reshape(1, d), w_up.astype(BF16), w_conv.astype(F32),
                     b_conv.astype(F32).reshape(1, -1), w_down.astype(BF16),
                     _pick(seq, (1024, 512, 256)), 256)


def kernel(x, ln_mix, ln_ffn, ev_w_in, ev_w_out, ev_a_qnorm, ev_a_knorm, ev_idx_knorm, ev_b_qnorm, ev_b_knorm, ev_lam_q1, ev_lam_k1, ev_lam_q2, ev_lam_k2, ev_b_subln, od_w_in, od_b_f, od_w_out, od_c_qnorm, od_c_knorm, ffn_up, ffn_conv, ffn_conv_b, ffn_down):
    batch, seq, d = x.shape
    depth = ln_mix.shape[0]
    h = x.reshape(batch * seq, d)
    for i in range(depth):
        j = i // 2
        if i % 2 == 0:
            h = _even_layer(h, batch, seq, i, ln_mix[i], ev_w_in[j], ev_w_out[j], ev_a_qnorm[j],
                            ev_a_knorm[j], ev_idx_knorm[j], ev_b_qnorm[j], ev_b_knorm[j],
                            ev_lam_q1[j], ev_lam_k1[j], ev_lam_q2[j], ev_lam_k2[j], ev_b_subln[j])
        else:
            h = _odd_layer(h, batch, seq, ln_mix[i], od_w_in[j], od_b_f[j], od_w_out[j],
                           od_c_qnorm[j], od_c_knorm[j])
        h = _ffn_layer(h, seq, ln_ffn[i], ffn_up[i], ffn_conv[i], ffn_conv_b[i], ffn_down[i])
    return h.reshape(batch, seq, d)
```

```python
import functools
import math

import jax
import jax.numpy as jnp
import numpy as np
from jax import lax
from jax.experimental import pallas as pl
from jax.experimental.pallas import tpu as pltpu

F32 = jnp.float32
BF16 = jnp.bfloat16
I32 = jnp.int32

CHUNK = 64
HEAD_DIM = 64
ROT_FRAC = 4
ROPE_THETA = 500000.0
NORM_EPS = 1e-6
NEG_INF = -1e30
A_HEADS = 8
IDX_HEADS = 8
IDX_DIM = 32
TOPK_MAX = 256
B_HEADS = 4
CONV_WIDTH = 3

LANES = 128
INT_MIN = -2147483648
VMEM_LIMIT = 56 * 1024 * 1024
LOG2E = 1.4426950408889634
AUX = HEAD_DIM
NORM_MARGIN = 1.01
SAFE_SPAN = 100.0

_NT = (((1,), (1,)), ((), ()))


def _params(sem):
    return pltpu.CompilerParams(dimension_semantics=sem, vmem_limit_bytes=VMEM_LIMIT)


def _rms_matmul_body(x_ref, g_ref, w_ref, o_ref, h_ref):
    @pl.when(pl.program_id(1) == 0)
    def _():
        x = x_ref[...]
        ms = jnp.mean(x * x, axis=-1, keepdims=True)
        h_ref[...] = (x * lax.rsqrt(ms + NORM_EPS) * g_ref[...]).astype(BF16)

    o_ref[...] = jnp.dot(h_ref[...], w_ref[...], preferred_element_type=F32)


def _rms_matmul(x, g, w, tm, tn):
    m, k = x.shape
    n = w.shape[1]
    return pl.pallas_call(
        _rms_matmul_body,
        grid=(m // tm, n // tn),
        in_specs=[
            pl.BlockSpec((tm, k), lambda i, j: (i, 0)),
            pl.BlockSpec((1, k), lambda i, j: (0, 0)),
            pl.BlockSpec((k, tn), lambda i, j: (0, j)),
        ],
        out_specs=pl.BlockSpec((tm, tn), lambda i, j: (i, j)),
        out_shape=jax.ShapeDtypeStruct((m, n), F32),
        scratch_shapes=[pltpu.VMEM((tm, k), BF16)],
        compiler_params=_params(("parallel", "arbitrary")),
    )(x, g, w)


def _proj_residual_body(*refs, n_pairs):
    res_ref = refs[0]
    o_ref = refs[1 + 2 * n_pairs]
    acc = res_ref[...]
    for t in range(n_pairs):
        acc = acc + jnp.dot(refs[1 + 2 * t][...], refs[2 + 2 * t][...], preferred_element_type=F32)
    o_ref[...] = acc


def _proj_residual(res, pairs, tm):
    m, n = res.shape
    in_specs = [pl.BlockSpec((tm, n), lambda i: (i, 0))]
    args = [res]
    for a, w in pairs:
        in_specs.append(pl.BlockSpec((tm, a.shape[1]), lambda i: (i, 0)))
        in_specs.append(pl.BlockSpec(w.shape, lambda i: (0, 0)))
        args += [a, w]
    return pl.pallas_call(
        functools.partial(_proj_residual_body, n_pairs=len(pairs)),
        grid=(m // tm,),
        in_specs=in_specs,
        out_specs=pl.BlockSpec((tm, n), lambda i: (i, 0)),
        out_shape=jax.ShapeDtypeStruct((m, n), F32),
        compiler_params=_params(("parallel",)),
    )(*args)


def _split_bf16(x):
    hi = x.astype(BF16)
    lo = (x - hi.astype(F32)).astype(BF16)
    return hi, lo


def _pieces3(x):
    p1 = x.astype(BF16).astype(F32)
    r = x - p1
    p2 = r.astype(BF16).astype(F32)
    p3 = (r - p2).astype(BF16).astype(F32)
    return p1, p2, p3


def _head_rms(xc, bd, gain):
    hi, lo = _split_bf16(xc * xc)
    ms = jnp.dot(hi, bd, preferred_element_type=F32) + jnp.dot(lo, bd, preferred_element_type=F32)
    return xc * lax.rsqrt(ms + NORM_EPS) * gain


def _rope(y, c, a, b, shift):
    return y * c + pltpu.roll(y, LANES - shift, 1) * a + pltpu.roll(y, shift, 1) * b


def _rope_tables(seq, head, reps):
    rot = head // ROT_FRAC
    half = rot // 2
    pos = jnp.arange(seq, dtype=jnp.int32)
    inv = ROPE_THETA ** (-jnp.arange(half, dtype=F32) / half)
    ang = pos.astype(F32)[:, None] * inv[None, :]
    cos, sin = jnp.cos(ang), jnp.sin(ang)
    ones = jnp.ones((seq, head - rot), F32)
    zeros = jnp.zeros((seq, head - rot), F32)
    zh = jnp.zeros((seq, half), F32)
    c = jnp.concatenate([cos, cos, ones], axis=1)
    a = jnp.concatenate([-sin, zh, zeros], axis=1)
    b = jnp.concatenate([zh, sin, zeros], axis=1)
    return tuple(jnp.tile(t, (1, reps)) for t in (c, a, b))


def _split_units(y, lane):
    lo = lane < HEAD_DIM
    return jnp.where(lo, y, 0.0), jnp.where(lo, pltpu.roll(y, HEAD_DIM, 1), 0.0)


def _track_max(max_ref, r, val, first):
    new = jnp.broadcast_to(val, (1, LANES))
    max_ref[r: r + 1, :] = jnp.where(first, new, jnp.maximum(max_ref[r: r + 1, :], new))


def _sq_norm_max(u):
    return jnp.max(jnp.sum(u * u, axis=1, keepdims=True), axis=0, keepdims=True)


def _prep_even_body(p_ref, c64_ref, a64_ref, b64_ref, c32_ref, a32_ref, b32_ref,
                    aqn_ref, akn_ref, bqn_ref, bkn_ref, ikn_ref, bd_ref,
                    pqh_ref, pql_ref, pkh_ref, pkl_ref,
                    aq_ref, ak_ref, av_ref, bq_ref, bk_ref, bv_ref, iqx_ref, ikx_ref, iw_ref,
                    amax_ref, bmax_ref, *, qscale, iw_scale, nt):
    bd = bd_ref[...]
    c64, a64, b64 = c64_ref[...], a64_ref[...], b64_ref[...]
    c32, a32, b32 = c32_ref[...], a32_ref[...], b32_ref[...]
    lane = lax.broadcasted_iota(I32, c64.shape, 1)
    k_aux = jnp.where((lane >= AUX) & (lane < AUX + 3), 1.0, 0.0)
    v_aux = jnp.where(lane == AUX, 1.0, 0.0)
    first = pl.program_id(0) % nt == 0

    def normed(col0, g, gain):
        xc = p_ref[:, col0 + LANES * g: col0 + LANES * (g + 1)]
        return _rope(_head_rms(xc, bd, gain), c64, a64, b64, 8)

    def q_side(col0, gain_ref, out_ref):
        gain = gain_ref[...]
        for g in range(4):
            for t, u in enumerate(_split_units(normed(col0, g, gain) * qscale, lane)):
                out_ref[:, LANES * (2 * g + t): LANES * (2 * g + t + 1)] = u.astype(BF16)

    def k_side(col0, gain_ref, out_ref, max_ref):
        gain = gain_ref[...]
        for g in range(4):
            for t, u in enumerate(_split_units(normed(col0, g, gain), lane)):
                _track_max(max_ref, 2 * g + t, _sq_norm_max(u), first)
                out_ref[:, LANES * (2 * g + t): LANES * (2 * g + t + 1)] = (u + k_aux).astype(BF16)

    q_side(0, aqn_ref, aq_ref)
    k_side(512, akn_ref, ak_ref, amax_ref)
    for g in range(4):
        for t, u in enumerate(_split_units(p_ref[:, 1024 + LANES * g: 1024 + LANES * (g + 1)], lane)):
            av_ref[:, LANES * (2 * g + t): LANES * (2 * g + t + 1)] = (u + v_aux).astype(BF16)
    q_side(1536, bqn_ref, bq_ref)
    k_side(2048, bkn_ref, bk_ref, bmax_ref)
    bv_ref[...] = p_ref[:, 2560:3072].astype(BF16)

    g0 = _rope(p_ref[:, 3072:3200], c32, a32, b32, 4)
    g1 = _rope(p_ref[:, 3200:3328], c32, a32, b32, 4)
    g2 = p_ref[:, 3328:3456]
    is_k = lane < IDX_DIM
    ms = jnp.sum(jnp.where(is_k, g2 * g2, 0.0), axis=1, keepdims=True) * (1.0 / IDX_DIM)
    g2 = jnp.where(is_k, g2 * lax.rsqrt(ms + NORM_EPS) * ikn_ref[...], g2)
    g2 = _rope(g2, jnp.where(is_k, c32, 1.0), jnp.where(is_k, a32, 0.0), jnp.where(is_k, b32, 0.0), 4)
    iw_ref[...] = g2 * iw_scale
    hi, lo = _split_bf16(jnp.concatenate([g0, g1, g2], axis=1))
    iqx_ref[...] = (jnp.dot(hi, pqh_ref[...], preferred_element_type=F32)
                    + jnp.dot(lo, pql_ref[...], preferred_element_type=F32)).astype(BF16)
    ikx_ref[...] = (jnp.dot(hi, pkh_ref[...], preferred_element_type=F32)
                    + jnp.dot(lo, pkl_ref[...], preferred_element_type=F32)).astype(BF16)


def _index_placement():
    pqh = np.zeros((384, IDX_HEADS * LANES), np.float32)
    pql = np.zeros((384, IDX_HEADS * LANES), np.float32)
    for h in range(IDX_HEADS):
        for d in range(IDX_DIM):
            pqh[IDX_DIM * h + d, LANES * h + d] = 1.0
            pqh[IDX_DIM * h + d, LANES * h + IDX_DIM + d] = 1.0
            pql[IDX_DIM * h + d, LANES * h + 2 * IDX_DIM + d] = 1.0
    pkh = np.zeros((384, LANES), np.float32)
    pkl = np.zeros((384, LANES), np.float32)
    for d in range(IDX_DIM):
        pkh[256 + d, d] = 1.0
        pkh[256 + d, 2 * IDX_DIM + d] = 1.0
        pkl[256 + d, IDX_DIM + d] = 1.0
    return tuple(jnp.asarray(t, BF16) for t in (pqh, pql, pkh, pkl))


def _block_diag_mean():
    r = np.arange(LANES)
    return jnp.asarray((r[:, None] // HEAD_DIM == r[None, :] // HEAD_DIM) / HEAD_DIM, BF16)


def _prep_even(proj, batch, seq, tabs64, tabs32, aqn, akn, bqn, bkn, ikn, tm):
    m = proj.shape[0]
    nt = seq // tm
    row = lambda w: pl.BlockSpec((tm, w), lambda i: (i, 0))
    tab = pl.BlockSpec((tm, LANES), lambda i: (i % nt, 0))
    full = lambda a: pl.BlockSpec(a.shape, lambda i: (0, 0))
    per_seq = pl.BlockSpec((None, 8, LANES), lambda i: (i // nt, 0, 0))
    bd = _block_diag_mean()
    place = _index_placement()
    small = [aqn, akn, bqn, bkn, ikn, bd, *place]
    out_w = [1024, 1024, 1024, 1024, 1024, 512, IDX_HEADS * LANES, LANES, LANES]
    out_dt = [BF16] * 8 + [F32]
    return pl.pallas_call(
        functools.partial(_prep_even_body, qscale=HEAD_DIM ** -0.5 * LOG2E,
                          iw_scale=IDX_HEADS ** -0.5 * IDX_DIM ** -0.5, nt=nt),
        grid=(m // tm,),
        in_specs=[row(proj.shape[1])] + [tab] * 6 + [full(a) for a in small],
        out_specs=[row(w) for w in out_w] + [per_seq, per_seq],
        out_shape=[jax.ShapeDtypeStruct((m, w), dt) for w, dt in zip(out_w, out_dt)]
                  + [jax.ShapeDtypeStruct((batch, 8, LANES), F32)] * 2,
        compiler_params=_params(("arbitrary",)),
    )(proj, *tabs64, *tabs32, *small)


def _prep_odd_body(p_ref, qn_ref, kn_ref, bf_ref, bd_ref, pl1_ref, pl2_ref, pl3_ref,
                   q_ref, k_ref, v_ref, c_ref, kmax_ref, carry_ref, *, qscale, nt, tm):
    bd = bd_ref[...]
    qn, kn = qn_ref[...], kn_ref[...]
    lane = lax.broadcasted_iota(I32, (tm, LANES), 1)
    k_aux = jnp.where((lane >= AUX) & (lane < AUX + 3), 1.0, 0.0)
    v_aux = jnp.where(lane == AUX, 1.0, 0.0)
    first = pl.program_id(0) % nt == 0

    @pl.when(first)
    def _():
        carry_ref[...] = jnp.zeros_like(carry_ref)

    z = p_ref[:, 3072:3200] + bf_ref[...]
    logf = jnp.minimum(z, 0.0) - jnp.log1p(jnp.exp(-jnp.abs(z)))
    r = lax.broadcasted_iota(I32, (tm, tm), 0)
    cidx = lax.broadcasted_iota(I32, (tm, tm), 1)
    tri = jnp.where(r >= cidx, 1.0, 0.0).astype(BF16)
    f1, f2, f3 = (t.astype(BF16) for t in _pieces3(logf))
    cs = (jnp.dot(tri, f1, preferred_element_type=F32) + jnp.dot(tri, f2, preferred_element_type=F32)
          + jnp.dot(tri, f3, preferred_element_type=F32)) + carry_ref[...]
    carry_ref[...] = cs[tm - 1: tm, :]
    c2 = cs * LOG2E
    c_ref[...] = c2
    b1, b2, b3 = (t.astype(BF16) for t in _pieces3(c2))
    key_bias = (jnp.dot(b1, pl1_ref[...], preferred_element_type=F32)
                + jnp.dot(b2, pl2_ref[...], preferred_element_type=F32)
                + jnp.dot(b3, pl3_ref[...], preferred_element_type=F32))

    for g in range(8):
        sl = slice(LANES * g, LANES * (g + 1))
        for t, u in enumerate(_split_units(_head_rms(p_ref[:, sl], bd, qn) * qscale, lane)):
            q_ref[:, LANES * (2 * g + t): LANES * (2 * g + t + 1)] = u.astype(BF16)
        kk = _head_rms(p_ref[:, 1024 + LANES * g: 1024 + LANES * (g + 1)], bd, kn)
        for t, u in enumerate(_split_units(kk, lane)):
            usl = slice(LANES * (2 * g + t), LANES * (2 * g + t + 1))
            _track_max(kmax_ref, 2 * g + t, _sq_norm_max(u), first)
            k_ref[:, usl] = (u + k_aux + key_bias[:, usl]).astype(BF16)
        for t, u in enumerate(_split_units(p_ref[:, 2048 + LANES * g: 2048 + LANES * (g + 1)], lane)):
            v_ref[:, LANES * (2 * g + t): LANES * (2 * g + t + 1)] = (u + v_aux).astype(BF16)


def _key_bias_placement(heads):
    mats = []
    for i in range(3):
        p = np.zeros((LANES, heads * LANES), np.float32)
        for h in range(heads):
            p[h, LANES * h + AUX + 3 + i] = -1.0
        mats.append(jnp.asarray(p, BF16))
    return mats


def _prep_odd(proj, batch, seq, qn, kn, bf, tm):
    m = proj.shape[0]
    nt = seq // tm
    heads = 1024 // HEAD_DIM
    row = lambda w: pl.BlockSpec((tm, w), lambda i: (i, 0))
    full = lambda a: pl.BlockSpec(a.shape, lambda i: (0, 0))
    small = [qn, kn, bf, _block_diag_mean(), *_key_bias_placement(heads)]
    return pl.pallas_call(
        functools.partial(_prep_odd_body, qscale=HEAD_DIM ** -0.5 * LOG2E, nt=nt, tm=tm),
        grid=(m // tm,),
        in_specs=[row(proj.shape[1])] + [full(a) for a in small],
        out_specs=[row(2048), row(2048), row(2048), row(LANES),
                   pl.BlockSpec((None, heads, LANES), lambda i: (i // nt, 0, 0))],
        out_shape=[jax.ShapeDtypeStruct((m, 2048), BF16)] * 3
                  + [jax.ShapeDtypeStruct((m, LANES), F32),
                     jax.ShapeDtypeStruct((batch, heads, LANES), F32)],
        scratch_shapes=[pltpu.VMEM((1, LANES), F32)],
        compiler_params=_params(("arbitrary",)),
    )(proj, *small)


def _unit_q(qd, shift, lane):
    a1, a2, a3 = _pieces3(shift)
    aux = jnp.where(lane == AUX, a1, jnp.where(lane == AUX + 1, a2, jnp.where(
        lane == AUX + 2, a3, jnp.where(lane < AUX + 6, 1.0, 0.0))))
    return jnp.where(lane < AUX, qd, aux).astype(BF16)


def _logit_bound(qd, kmax2):
    return jnp.sqrt(jnp.sum(qd * qd, axis=1, keepdims=True) * kmax2) * NORM_MARGIN


def _lane_fold(x):
    part = x[:, :LANES]
    for f in range(1, x.shape[1] // LANES):
        part = part + x[:, LANES * f: LANES * (f + 1)]
    return part


def _flash_body(*refs, mode, t, lam_init):
    if mode == "fox":
        (q_ref, k_ref, v_ref, c_ref, kmax_ref, o_ref, qx_ref, shift_ref, acc_ref) = refs
    else:
        (q_ref, k_ref, v_ref, kmax_ref, lq1_ref, lk1_ref, lq2_ref, lk2_ref, sub_ref,
         o_ref, qx_ref, shift_ref, acc_ref, lsum_ref) = refs
    pair = pl.program_id(1)
    i = pl.program_id(2)
    lane = lax.broadcasted_iota(I32, (t, LANES), 1)
    row = lax.broadcasted_iota(I32, (t, t), 0)
    col = lax.broadcasted_iota(I32, (t, t), 1)
    diag_ok = (col <= row) if mode == "fox" else ((col // CHUNK) <= (row // CHUNK))

    def usl(u):
        return slice(LANES * u, LANES * (u + 1))

    def logits(qx, j, u, masked):
        off = pl.multiple_of(j * t, t)
        s = lax.dot_general(qx, k_ref[pl.ds(off, t), usl(u)], _NT, preferred_element_type=F32)
        return jnp.where(diag_ok, s, NEG_INF) if masked else s

    qd, base, bound = [], [], []
    for u in range(2):
        qd.append(q_ref[:, usl(u)].astype(F32))
        if mode == "fox":
            base.append(jnp.sum(jnp.where(lane == 2 * pair + u, c_ref[...], 0.0), axis=1, keepdims=True))
        else:
            base.append(jnp.zeros((t, 1), F32))
        bound.append(_logit_bound(qd[u], kmax_ref[pl.ds(2 * pair + u, 1), 0:1]))
        shift_ref[u] = base[u] - bound[u]
    safe = 2.0 * jnp.maximum(jnp.max(bound[0]), jnp.max(bound[1])) <= SAFE_SPAN

    @pl.when(jnp.logical_not(safe))
    def _():
        for u in range(2):
            qx = _unit_q(qd[u], base[u], lane)

            def mbody(j, m):
                return jnp.maximum(m, jnp.max(logits(qx, j, u, False), axis=1, keepdims=True))

            m = lax.fori_loop(0, i, mbody, jnp.full((t, 1), NEG_INF, F32))
            m = jnp.maximum(m, jnp.max(logits(qx, i, u, True), axis=1, keepdims=True))
            shift_ref[u] = base[u] - m

    for u in range(2):
        qx_ref[u] = _unit_q(qd[u], shift_ref[u], lane)
    acc_ref[...] = jnp.zeros(acc_ref.shape, F32)
    if mode == "diff":
        lsum_ref[...] = jnp.zeros(lsum_ref.shape, F32)

    def step(j, masked):
        off = pl.multiple_of(j * t, t)
        for u in range(2):
            p = jnp.exp2(logits(qx_ref[u], j, u, masked))
            v = v_ref[pl.ds(off, t), usl(u)] if mode == "fox" else v_ref[pl.ds(off, t), :]
            if mode == "diff":
                lsum_ref[u] += _lane_fold(p)
            acc_ref[u] += jnp.dot(p.astype(BF16), v, preferred_element_type=F32)

    def body(j, carry):
        step(j, False)
        return carry

    lax.fori_loop(0, i, body, 0)
    step(i, True)

    if mode == "fox":
        o0 = acc_ref[0] / acc_ref[0][:, AUX: AUX + 1]
        o1 = acc_ref[1] / acc_ref[1][:, AUX: AUX + 1]
        o_ref[...] = jnp.where(lane < HEAD_DIM, o0, pltpu.roll(o1, HEAD_DIM, 1)).astype(BF16)
    else:
        o0 = acc_ref[0] / jnp.sum(lsum_ref[0], axis=1, keepdims=True)
        o1 = acc_ref[1] / jnp.sum(lsum_ref[1], axis=1, keepdims=True)
        lam = (jnp.exp(jnp.sum(lq1_ref[...] * lk1_ref[...], axis=1, keepdims=True))
               - jnp.exp(jnp.sum(lq2_ref[...] * lk2_ref[...], axis=1, keepdims=True)) + lam_init)
        d = o0 - lam * o1
        ms = jnp.mean(d * d, axis=1, keepdims=True)
        o_ref[...] = (d * lax.rsqrt(ms + NORM_EPS) * sub_ref[...] * (1.0 - lam_init)).astype(BF16)


def _flash(mode, q, k, v, kmax, extra, batch, seq, lam_init=0.0, t=512):
    m = q.shape[0]
    pairs = q.shape[1] // (2 * LANES)
    t = min(t, seq)
    nq = seq // t
    qspec = pl.BlockSpec((t, 2 * LANES), lambda b, p, i: (b * nq + i, p))
    kspec = pl.BlockSpec((seq, 2 * LANES), lambda b, p, i: (b, p))
    mspec = pl.BlockSpec((None, kmax.shape[1], LANES), lambda b, p, i: (b, 0, 0))
    scratch = [pltpu.VMEM((2, t, LANES), BF16), pltpu.VMEM((2, t, 1), F32), pltpu.VMEM((2, t, LANES), F32)]
    if mode == "fox":
        in_specs = [qspec, kspec, kspec, pl.BlockSpec((t, LANES), lambda b, p, i: (b * nq + i, 0)), mspec]
        args = [q, k, v, extra, kmax]
    else:
        in_specs = ([qspec, kspec, pl.BlockSpec((seq, LANES), lambda b, p, i: (b, p)), mspec]
                    + [pl.BlockSpec(a.shape, lambda b, p, i: (0, 0)) for a in extra])
        args = [q, k, v, kmax, *extra]
        scratch.append(pltpu.VMEM((2, t, LANES), F32))
    return pl.pallas_call(
        functools.partial(_flash_body, mode=mode, t=t, lam_init=lam_init),
        grid=(batch, pairs, nq),
        in_specs=in_specs,
        out_specs=pl.BlockSpec((t, LANES), lambda b, p, i: (b * nq + i, p)),
        out_shape=jax.ShapeDtypeStruct((m, pairs * LANES), BF16),
        scratch_shapes=scratch,
        compiler_params=_params(("parallel", "parallel", "arbitrary")),
    )(*args)


def _dsa_body(iqx_ref, iw_ref, aq_ref, amax_ref, ikx_ref, ak_ref, av_ref, o_ref,
              key_ref, qx_ref, shift_ref, acc_ref, *, tq, tk, topk, idx_bits):
    i = pl.program_id(1)
    nkt = (i * tq) // tk + 1

    w = iw_ref[...]

    def score_tile(kt, masked):
        off = pl.multiple_of(kt * tk, tk)
        kx = ikx_ref[pl.ds(off, tk), :]
        sc = jnp.zeros((tq, tk), F32)
        for h in range(IDX_HEADS):
            lg = lax.dot_general(iqx_ref[:, LANES * h: LANES * (h + 1)], kx, _NT,
                                 preferred_element_type=F32)
            sc = sc + jnp.maximum(lg, 0.0) * w[:, IDX_DIM + h: IDX_DIM + h + 1]
        bits = pltpu.bitcast(sc, I32)
        key = jnp.where(bits < 0, bits ^ 0x7FFFFFFF, bits)
        if masked:
            row = lax.broadcasted_iota(I32, (tq, tk), 0)
            col = lax.broadcasted_iota(I32, (tq, tk), 1) + off
            adm = (col // CHUNK) <= ((i * tq + row) // CHUNK)
            key = jnp.where(adm, key, INT_MIN)
        key_ref[:, pl.ds(off, tk)] = key

    def score_body(kt, carry):
        score_tile(kt, False)
        return carry

    lax.fori_loop(0, nkt - 1, score_body, 0)
    score_tile(nkt - 1, True)

    def count(pred):
        def body(kt, c):
            off = pl.multiple_of(kt * tk, tk)
            return c + _lane_fold(jnp.where(pred(key_ref[:, pl.ds(off, tk)], off), 1.0, 0.0))
        c = lax.fori_loop(0, nkt, body, jnp.zeros((tq, LANES), F32))
        return jnp.sum(c, axis=1, keepdims=True)

    def bit_body(b, carry):
        res, cnt_res = carry
        cand = res | jnp.left_shift(jnp.int32(1), 31 - b)
        cand_s = cand ^ INT_MIN
        cnt = count(lambda kk, off: kk >= cand_s)
        ok = cnt >= topk
        return jnp.where(ok, cand, res), jnp.where(ok, cnt, cnt_res)

    res, cnt_res = lax.fori_loop(0, 32, bit_body,
                                 (jnp.zeros((tq, 1), I32), jnp.zeros((tq, 1), F32)))
    thr = jnp.maximum(res ^ INT_MIN, INT_MIN + 1)

    @pl.when(jnp.max(cnt_res) > topk)
    def _():
        need = topk - count(lambda kk, off: kk > thr)

        def idx_of(off):
            return lax.broadcasted_iota(I32, (tq, tk), 1) + off

        def jbody(b, lo):
            cand = lo | jnp.left_shift(jnp.int32(1), idx_bits - 1 - b)
            c = count(lambda kk, off: (kk == thr) & (idx_of(off) < cand))
            return jnp.where(c < need, cand, lo)

        last = lax.fori_loop(0, idx_bits, jbody, jnp.zeros((tq, 1), I32))

        def drop(kt, carry):
            off = pl.multiple_of(kt * tk, tk)
            kk = key_ref[:, pl.ds(off, tk)]
            key_ref[:, pl.ds(off, tk)] = jnp.where((kk == thr) & (idx_of(off) > last), INT_MIN, kk)
            return carry

        lax.fori_loop(0, nkt, drop, 0)

    lane = lax.broadcasted_iota(I32, (tq, LANES), 1)

    def usl(u):
        return slice(LANES * u, LANES * (u + 1))

    def selected(off):
        return key_ref[:, pl.ds(off, tk)] >= thr

    def logits(qx, off, u, sel):
        s = lax.dot_general(qx, ak_ref[pl.ds(off, tk), usl(u)], _NT, preferred_element_type=F32)
        return jnp.where(sel, s, NEG_INF)

    qd, bound = [], []
    for u in range(A_HEADS):
        qd.append(aq_ref[:, usl(u)].astype(F32))
        bound.append(_logit_bound(qd[u], amax_ref[u: u + 1, 0:1]))
        shift_ref[u] = -bound[u]
    worst = bound[0]
    for u in range(1, A_HEADS):
        worst = jnp.maximum(worst, bound[u])
    safe = 2.0 * jnp.max(worst) <= SAFE_SPAN

    @pl.when(jnp.logical_not(safe))
    def _():
        zero = jnp.zeros((tq, 1), F32)
        for u in range(A_HEADS):
            qx = _unit_q(qd[u], zero, lane)

            def mbody(kt, m):
                off = pl.multiple_of(kt * tk, tk)
                return jnp.maximum(m, jnp.max(logits(qx, off, u, selected(off)), axis=1, keepdims=True))

            shift_ref[u] = -lax.fori_loop(0, nkt, mbody, jnp.full((tq, 1), NEG_INF, F32))

    for u in range(A_HEADS):
        qx_ref[u] = _unit_q(qd[u], shift_ref[u], lane)
    acc_ref[...] = jnp.zeros(acc_ref.shape, F32)

    def attend(kt, carry):
        off = pl.multiple_of(kt * tk, tk)
        sel = selected(off)
        for u in range(A_HEADS):
            p = jnp.exp2(logits(qx_ref[u], off, u, sel))
            acc_ref[u] += jnp.dot(p.astype(BF16), av_ref[pl.ds(off, tk), usl(u)],
                                  preferred_element_type=F32)
        return carry

    lax.fori_loop(0, nkt, attend, 0)
    for p in range(A_HEADS // 2):
        o0 = acc_ref[2 * p] / acc_ref[2 * p][:, AUX: AUX + 1]
        o1 = acc_ref[2 * p + 1] / acc_ref[2 * p + 1][:, AUX: AUX + 1]
        o_ref[:, usl(p)] = jnp.where(lane < HEAD_DIM, o0, pltpu.roll(o1, HEAD_DIM, 1)).astype(BF16)


def _dsa(iqx, iw, aq, amax, ikx, ak, av, batch, seq, tq=128, tk=512):
    m = aq.shape[0]
    nq = seq // tq
    tk = min(tk, seq)
    topk = min(TOPK_MAX, seq // 4)
    qrow = lambda w: pl.BlockSpec((tq, w), lambda b, i: (b * nq + i, 0))
    krow = lambda w: pl.BlockSpec((seq, w), lambda b, i: (b, 0), pipeline_mode=pl.Buffered(1))
    return pl.pallas_call(
        functools.partial(_dsa_body, tq=tq, tk=tk, topk=float(topk),
                          idx_bits=max(1, (seq - 1).bit_length())),
        grid=(batch, nq),
        in_specs=[qrow(iqx.shape[1]), qrow(LANES), qrow(aq.shape[1]),
                  pl.BlockSpec((None, 8, LANES), lambda b, i: (b, 0, 0)),
                  krow(LANES), krow(ak.shape[1]), krow(av.shape[1])],
        out_specs=qrow(A_HEADS * HEAD_DIM),
        out_shape=jax.ShapeDtypeStruct((m, A_HEADS * HEAD_DIM), BF16),
        scratch_shapes=[pltpu.VMEM((tq, seq), I32), pltpu.VMEM((A_HEADS, tq, LANES), BF16),
                        pltpu.VMEM((A_HEADS, tq, 1), F32), pltpu.VMEM((A_HEADS, tq, LANES), F32)],
        compiler_params=_params(("parallel", "arbitrary")),
    )(iqx, iw, aq, amax, ikx, ak, av)


HALO = 8


def _ffn_body(x_ref, halo_ref, g_ref, wg_ref, wv_ref, cg_ref, cv_ref, bg_ref, bv_ref, wd_ref,
              o_ref, h_ref, ug_ref, uv_ref, acc_ref, *, tm, nt):
    i = pl.program_id(0)
    f = pl.program_id(1)

    @pl.when(f == 0)
    def _():
        def norm(x):
            ms = jnp.mean(x * x, axis=-1, keepdims=True)
            return x * lax.rsqrt(ms + NORM_EPS) * g_ref[...]

        first = (i % nt) == 0
        h_ref[0:HALO, :] = jnp.where(first, 0.0, norm(halo_ref[...])).astype(BF16)
        h_ref[HALO:, :] = norm(x_ref[...]).astype(BF16)
        acc_ref[...] = jnp.zeros_like(acc_ref)

    h = h_ref[...]
    ug_ref[...] = jnp.dot(h, wg_ref[...], preferred_element_type=F32)
    uv_ref[...] = jnp.dot(h, wv_ref[...], preferred_element_type=F32)

    def conv(u_ref, c_ref, b_ref):
        out = b_ref[...]
        for j in range(CONV_WIDTH):
            start = HALO - (CONV_WIDTH - 1) + j
            out = out + u_ref[start: start + tm, :] * c_ref[j: j + 1, :]
        return out

    g = conv(ug_ref, cg_ref, bg_ref)
    val = conv(uv_ref, cv_ref, bv_ref)
    act = (g / (1.0 + jnp.exp(-g))) * val
    acc_ref[...] += jnp.dot(act.astype(BF16), wd_ref[...], preferred_element_type=F32)

    @pl.when(f == pl.num_programs(1) - 1)
    def _():
        o_ref[...] = x_ref[...] + acc_ref[...]


def _conv_ffn(x, seq, g, w_up, w_conv, b_conv, w_down, tm, tf):
    m, d = x.shape
    dff = w_down.shape[0]
    nf = dff // tf
    nt = seq // tm
    hb = tm // HALO
    return pl.pallas_call(
        functools.partial(_ffn_body, tm=tm, nt=nt),
        grid=(m // tm, nf),
        in_specs=[
            pl.BlockSpec((tm, d), lambda i, f: (i, 0)),
            pl.BlockSpec((HALO, d), lambda i, f: (jnp.maximum(i * hb - 1, 0), 0)),
            pl.BlockSpec((1, d), lambda i, f: (0, 0)),
            pl.BlockSpec((d, tf), lambda i, f: (0, f)),
            pl.BlockSpec((d, tf), lambda i, f: (0, nf + f)),
            pl.BlockSpec((CONV_WIDTH, tf), lambda i, f: (0, f)),
            pl.BlockSpec((CONV_WIDTH, tf), lambda i, f: (0, nf + f)),
            pl.BlockSpec((1, tf), lambda i, f: (0, f)),
            pl.BlockSpec((1, tf), lambda i, f: (0, nf + f)),
            pl.BlockSpec((tf, d), lambda i, f: (f, 0)),
        ],
        out_specs=pl.BlockSpec((tm, d), lambda i, f: (i, 0)),
        out_shape=jax.ShapeDtypeStruct((m, d), F32),
        scratch_shapes=[pltpu.VMEM((tm + HALO, d), BF16), pltpu.VMEM((tm + HALO, tf), F32),
                        pltpu.VMEM((tm + HALO, tf), F32), pltpu.VMEM((tm, d), F32)],
        compiler_params=_params(("parallel", "arbitrary")),
    )(x, x, g, w_up, w_up, w_conv, w_conv, b_conv, b_conv, w_down)


def _tile2(g):
    return jnp.tile(g.astype(F32), 2).reshape(1, LANES)


def _pick(total, prefs):
    for p in prefs:
        if total % p == 0:
            return p
    return total


def _even_layer(x, batch, seq, layer, ln, w_in, w_out, a_qn, a_kn, idx_kn, b_qn, b_kn,
                lq1, lk1, lq2, lk2, b_subln):
    d = x.shape[1]
    a_w = A_HEADS * HEAD_DIM
    iq_w = IDX_HEADS * IDX_DIM
    b_w = B_HEADS * 2 * HEAD_DIM
    cuts = np.cumsum([a_w, a_w, a_w, iq_w, IDX_DIM, IDX_HEADS, b_w, b_w])
    aq, ak, av, iq, ik, iw, bq, bk, bv = jnp.split(w_in, [int(c) for c in cuts], axis=1)
    pad = jnp.zeros((d, 384 - iq_w - IDX_DIM - IDX_HEADS), w_in.dtype)
    w_cat = jnp.concatenate([aq, ak, av, bq, bk, bv, iq, ik, iw, pad], axis=1).astype(BF16)

    tm = _pick(x.shape[0], (1024, 512, 256))
    proj = _rms_matmul(x, ln.reshape(1, d), w_cat, tm, _pick(w_cat.shape[1], (1152, 384, 128)))

    tabs64 = _rope_tables(seq, HEAD_DIM, LANES // HEAD_DIM)
    tabs32 = _rope_tables(seq, IDX_DIM, LANES // IDX_DIM)
    ikn = jnp.concatenate([idx_kn.astype(F32), jnp.ones((LANES - IDX_DIM,), F32)]).reshape(1, LANES)
    aq_, ak_, av_, bq_, bk_, bv_, iqx, ikx, iw_, amax, bmax = _prep_even(
        proj, batch, seq, tabs64, tabs32, _tile2(a_qn), _tile2(a_kn), _tile2(b_qn), _tile2(b_kn), ikn,
        _pick(seq, (256, 128)))

    a_out = _dsa(iqx, iw_, aq_, amax, ikx, ak_, av_, batch, seq)
    lam_init = 0.8 - 0.6 * math.exp(-0.3 * layer)
    lam_p = [t.astype(F32).reshape(1, HEAD_DIM) for t in (lq1, lk1, lq2, lk2)]
    b_out = _flash("diff", bq_, bk_, bv_, bmax, lam_p + [b_subln.astype(F32).reshape(1, LANES)],
                   batch, seq, lam_init=lam_init)
    w_o = w_out.astype(BF16)
    return _proj_residual(x, [(a_out, w_o[:a_w]), (b_out, w_o[a_w:])], _pick(x.shape[0], (512, 256)))


def _odd_layer(x, batch, seq, ln, w_in, b_f, w_out, c_qn, c_kn):
    d = x.shape[1]
    heads = d // HEAD_DIM
    pad = jnp.zeros((d, LANES - heads), w_in.dtype)
    w_cat = jnp.concatenate([w_in, pad], axis=1).astype(BF16)
    tm = _pick(x.shape[0], (1024, 512, 256))
    proj = _rms_matmul(x, ln.reshape(1, d), w_cat, tm, _pick(w_cat.shape[1], (640, 128)))
    bf = jnp.concatenate([b_f.astype(F32), jnp.zeros((LANES - heads,), F32)]).reshape(1, LANES)
    q, k, v, c, kmax = _prep_odd(proj, batch, seq, _tile2(c_qn), _tile2(c_kn), bf, _pick(seq, (256, 128)))
    o = _flash("fox", q, k, v, kmax, c, batch, seq)
    return _proj_residual(x, [(o, w_out.astype(BF16))], _pick(x.shape[0], (512, 256)))


def _ffn_layer(x, seq, ln, w_up, w_conv, b_conv, w_down):
    d = x.shape[1]
    return _conv_ffn(x, seq, ln.reshape(1, d), w_up.astype(BF16), w_conv.astype(F32),
                     b_conv.astype(F32).reshape(1, -1), w_down.astype(BF16),
                     _pick(seq, (1024, 512, 256)), 256)


def kernel(x, ln_mix, ln_ffn, ev_w_in, ev_w_out, ev_a_qnorm, ev_a_knorm, ev_idx_knorm, ev_b_qnorm, ev_b_knorm, ev_lam_q1, ev_lam_k1, ev_lam_q2, ev_lam_k2, ev_b_subln, od_w_in, od_b_f, od_w_out, od_c_qnorm, od_c_knorm, ffn_up, ffn_conv, ffn_conv_b, ffn_down):
    batch, seq, d = x.shape
    depth = ln_mix.shape[0]
    h = x.reshape(batch * seq, d)
    for i in range(depth):
        j = i // 2
        if i % 2 == 0:
            h = _even_layer(h, batch, seq, i, ln_mix[i], ev_w_in[j], ev_w_out[j], ev_a_qnorm[j],
                            ev_a_knorm[j], ev_idx_knorm[j], ev_b_qnorm[j], ev_b_knorm[j],
                            ev_lam_q1[j], ev_lam_k1[j], ev_lam_q2[j], ev_lam_k2[j], ev_b_subln[j])
        else:
            h = _odd_layer(h, batch, seq, ln_mix[i], od_w_in[j], od_b_f[j], od_w_out[j],
                           od_c_qnorm[j], od_c_knorm[j])
        h = _ffn_layer(h, seq, ln_ffn[i], ffn_up[i], ffn_conv[i], ffn_conv_b[i], ffn_down[i])
    return h.reshape(batch, seq, d)
```
